```python
import jax, jax.numpy as jnp
from jax import lax
import numpy as np

D_MODEL = 2048
BATCH = 2
SEQ = 16384
DEPTH = 4

CHUNK = 64
N_A_LAYERS = DEPTH // 2
N_B_LAYERS = DEPTH - N_A_LAYERS
W_A = D_MODEL
POOL_WINDOWS = (2, 4, 8, 16)
N_POOL_GROUPS = len(POOL_WINDOWS)
G_A = W_A // N_POOL_GROUPS
W_B = D_MODEL
N_HEADS = 8
HEAD_DIM = W_B // N_HEADS
Q_BLOCK = 128
EPS = 1e-6

kernel_name = "yoco_pool_stickbreaking_trunk"


def _rmsnorm(x, g):
    xf = x.astype(jnp.float32)
    y = xf * lax.rsqrt(jnp.mean(xf * xf, axis=-1, keepdims=True) + EPS)
    return (y * g.astype(jnp.float32)).astype(x.dtype)


def _multiscale_pool(u, w_grp, scale):
    B, S, W = u.shape
    uf = u.astype(jnp.float32).reshape(B, S, N_POOL_GROUPS, G_A)
    cs = jnp.cumsum(uf, axis=1)
    pos = jnp.arange(S)
    outs = []
    for gi, w in enumerate(POOL_WINDOWS):
        csg = cs[:, :, gi]
        lo = jnp.pad(csg[:, :S - w], ((0, 0), (w, 0), (0, 0)))
        cnt = jnp.minimum(pos + 1, w).astype(jnp.float32)[None, :, None]
        outs.append((csg - lo) / cnt - uf[:, :, gi])
    pm = jnp.stack(outs, axis=2).astype(u.dtype)
    y = jnp.einsum('bsng,nge->bsne', pm, w_grp).reshape(B, S, W)
    return y * scale


def _stick_breaking(q, k, v):
    B, H, S, Dh = q.shape
    nb = S // Q_BLOCK
    inv_sqrt_d = 1.0 / float(np.sqrt(Dh))
    outs = []
    for bi in range(nb):
        start = bi * Q_BLOCK
        end = start + Q_BLOCK
        qblk = q[:, :, start:end]
        kk = k[:, :, :end]
        vv = v[:, :, :end]
        z = jnp.einsum('bhqd,bhkd->bhqk', qblk, kk).astype(jnp.float32) * inv_sqrt_d
        qpos = start + jnp.arange(Q_BLOCK)
        mask = jnp.arange(end)[None, :] < qpos[:, None]
        l = jnp.where(mask, jax.nn.log_sigmoid(-z), 0.0)
        rc = lax.cumsum(l, axis=3, reverse=True)
        a = jnp.where(mask, jnp.exp(z + rc), 0.0)
        outs.append(jnp.einsum('bhqk,bhkd->bhqd', a.astype(v.dtype), vv))
    return jnp.concatenate(outs, axis=2)


def _split_heads(t):
    B, S, _ = t.shape
    return t.reshape(B, S, N_HEADS, HEAD_DIM).transpose(0, 2, 1, 3)


def setup_inputs(seed: int = 0) -> dict:
    key = jax.random.key(seed)
    ks = jax.random.split(key, 13)
    f32 = jnp.float32
    nrm = lambda k, shape, fan_in: jax.random.normal(k, shape, f32) * (fan_in ** -0.5)
    return {
        "x": jax.random.normal(ks[0], (BATCH, SEQ, D_MODEL), f32),
        "a_norm": 1.0 + 0.02 * jax.random.normal(ks[1], (N_A_LAYERS, D_MODEL), f32),
        "a_w_in": nrm(ks[2], (N_A_LAYERS, D_MODEL, 2 * W_A), D_MODEL),
        "a_w_grp": nrm(ks[3], (N_A_LAYERS, N_POOL_GROUPS, G_A, G_A), G_A),
        "a_scale": 1.0 + 0.1 * jax.random.normal(ks[4], (N_A_LAYERS, W_A), f32),
        "a_w_out": nrm(ks[5], (N_A_LAYERS, W_A, D_MODEL), W_A),
        "kv_norm": 1.0 + 0.02 * jax.random.normal(ks[6], (D_MODEL,), f32),
        "w_kv": nrm(ks[7], (D_MODEL, 2 * W_B), D_MODEL),
        "b_norm": 1.0 + 0.02 * jax.random.normal(ks[8], (N_B_LAYERS, D_MODEL), f32),
        "b_w_in": nrm(ks[9], (N_B_LAYERS, D_MODEL, 2 * W_B), D_MODEL),
        "b_w_out": nrm(ks[10], (N_B_LAYERS, W_B, D_MODEL), W_B),
        "final_norm": 1.0 + 0.02 * jax.random.normal(ks[11], (D_MODEL,), f32),
    }


def reference(x, a_norm, a_w_in, a_w_grp, a_scale, a_w_out, kv_norm, w_kv,
              b_norm, b_w_in, b_w_out, final_norm):
    k_sh = None
    v_sh = None
    for i in range(DEPTH):
        if i < N_A_LAYERS:
            h = _rmsnorm(x, a_norm[i])
            u, z = jnp.split(h @ a_w_in[i], 2, axis=-1)
            y = _multiscale_pool(u, a_w_grp[i], a_scale[i]) * jax.nn.silu(z)
            x = x + y @ a_w_out[i]
            if i == N_A_LAYERS - 1:
                kv = _rmsnorm(x, kv_norm) @ w_kv
                k_all, v_all = jnp.split(kv, 2, axis=-1)
                k_sh = _split_heads(k_all)
                v_sh = _split_heads(v_all)
        else:
            j = i - N_A_LAYERS
            h = _rmsnorm(x, b_norm[j])
            q, z = jnp.split(h @ b_w_in[j], 2, axis=-1)
            o = _stick_breaking(_split_heads(q), k_sh, v_sh)
            B, H, S, Dh = o.shape
            o = o.transpose(0, 2, 1, 3).reshape(B, S, H * Dh)
            x = x + (o * jax.nn.silu(z)) @ b_w_out[j]
    return _rmsnorm(x, final_norm)
```

```python
import functools

import jax
import jax.numpy as jnp
from jax import lax
from jax.experimental import pallas as pl
from jax.experimental.pallas import tpu as pltpu

N_HEADS = 8
POOL_WINDOWS = (2, 4, 8, 16)
EPS = 1e-6

V7X_SUBLANES = 8
V7X_MXU_DIM = 256
V7X_VMEM_BYTES = 64 * 1024 * 1024
VMEM_LIMIT_BYTES = V7X_VMEM_BYTES - 8 * 1024 * 1024

POOL_HALO = -(-(max(POOL_WINDOWS) - 1) // V7X_SUBLANES) * V7X_SUBLANES
KEY_BLOCK = V7X_MXU_DIM

TOKEN_TILE = 512
QUERY_TILE = 512
PROJ_COL_CHUNK = 512

_F32 = jnp.float32
_BF16 = jnp.bfloat16


def _resident(block_shape, index_map):
    return pl.BlockSpec(block_shape, index_map, pipeline_mode=pl.Buffered(1))


def _params(*semantics):
    return pltpu.CompilerParams(dimension_semantics=semantics, vmem_limit_bytes=VMEM_LIMIT_BYTES)


def _rmsnorm(x, g):
    ms = jnp.mean(x * x, axis=-1, keepdims=True)
    return x * lax.rsqrt(ms + EPS) * g


def _silu(z):
    return z * jax.nn.sigmoid(z)


def _norm_proj_kernel(x_ref, g_ref, w_ref, o1_ref, o2_ref, h_ref, *, scale1):
    h_ref[...] = _rmsnorm(x_ref[...], g_ref[...]).astype(_BF16)
    n_half = o1_ref.shape[-1]
    for o_ref, col0, scale in ((o1_ref, 0, scale1), (o2_ref, n_half, 1.0)):
        for c in range(n_half // PROJ_COL_CHUNK):
            lo = c * PROJ_COL_CHUNK
            r = jnp.dot(h_ref[...], w_ref[:, col0 + lo:col0 + lo + PROJ_COL_CHUNK],
                        preferred_element_type=_F32)
            if scale != 1.0:
                r = r * scale
            o_ref[:, lo:lo + PROJ_COL_CHUNK] = r.astype(o_ref.dtype)


def _norm_proj(x, g, w, dtype1, dtype2, scale1=1.0):
    t, d = x.shape
    n_half = w.shape[1] // 2
    assert t % TOKEN_TILE == 0 and n_half % PROJ_COL_CHUNK == 0
    return pl.pallas_call(
        functools.partial(_norm_proj_kernel, scale1=scale1),
        grid=(t // TOKEN_TILE,),
        in_specs=[
            pl.BlockSpec((TOKEN_TILE, d), lambda i: (i, 0)),
            _resident((1, d), lambda i: (0, 0)),
            _resident(w.shape, lambda i: (0, 0)),
        ],
        out_specs=[
            pl.BlockSpec((TOKEN_TILE, n_half), lambda i: (i, 0)),
            pl.BlockSpec((TOKEN_TILE, n_half), lambda i: (i, 0)),
        ],
        out_shape=[jax.ShapeDtypeStruct((t, n_half), dtype1), jax.ShapeDtypeStruct((t, n_half), dtype2)],
        scratch_shapes=[pltpu.VMEM((TOKEN_TILE, d), _BF16)],
        compiler_params=_params("arbitrary"),
        name="norm_proj",
    )(x, g.reshape(1, d), w)


def _pool_gate_kernel(u_ref, z_ref, wg_ref, sc_ref, y_ref, ubuf):
    s = pl.program_id(1)
    tt = u_ref.shape[0]
    g = wg_ref.shape[-1]

    @pl.when(s == 0)
    def _():
        ubuf[0:POOL_HALO, :] = jnp.zeros((POOL_HALO, ubuf.shape[1]), _F32)

    @pl.when(s > 0)
    def _():
        ubuf[0:POOL_HALO, :] = ubuf[tt:tt + POOL_HALO, :]

    ubuf[POOL_HALO:POOL_HALO + tt, :] = u_ref[...]
    pos = s * tt + lax.broadcasted_iota(jnp.int32, (tt, 1), 0)
    for gi, w in enumerate(POOL_WINDOWS):
        cols = slice(gi * g, (gi + 1) * g)
        cur = ubuf[POOL_HALO:POOL_HALO + tt, cols]
        acc = cur
        for k in range(1, w):
            acc = acc + ubuf[POOL_HALO - k:POOL_HALO - k + tt, cols]
        cnt = jnp.minimum(pos + 1, w).astype(_F32)
        pm = acc / cnt - cur
        yg = jnp.dot(pm.astype(_BF16), wg_ref[gi], preferred_element_type=_F32)
        y_ref[:, cols] = (yg * sc_ref[:, cols] * _silu(z_ref[:, cols])).astype(y_ref.dtype)


def _pool_gate(u, z, w_grp, scale):
    b, s, w = u.shape
    assert s % TOKEN_TILE == 0 and TOKEN_TILE >= POOL_HALO
    tile = pl.BlockSpec((None, TOKEN_TILE, w), lambda bi, si: (bi, si, 0))
    return pl.pallas_call(
        _pool_gate_kernel,
        grid=(b, s // TOKEN_TILE),
        in_specs=[
            tile,
            tile,
            _resident(w_grp.shape, lambda bi, si: (0, 0, 0)),
            _resident((1, w), lambda bi, si: (0, 0)),
        ],
        out_specs=tile,
        out_shape=jax.ShapeDtypeStruct((b, s, w), _BF16),
        scratch_shapes=[pltpu.VMEM((POOL_HALO + TOKEN_TILE, w), _F32)],
        compiler_params=_params("arbitrary", "arbitrary"),
        name="pool_gate",
    )(u, z, w_grp, scale.reshape(1, w))


def _out_proj_kernel(x_ref, y_ref, w_ref, o_ref):
    o_ref[...] = x_ref[...] + jnp.dot(y_ref[...], w_ref[...], preferred_element_type=_F32)


def _out_proj(x, y, w):
    t, d = x.shape
    return pl.pallas_call(
        _out_proj_kernel,
        grid=(t // TOKEN_TILE,),
        in_specs=[
            pl.BlockSpec((TOKEN_TILE, d), lambda i: (i, 0)),
            pl.BlockSpec((TOKEN_TILE, w.shape[0]), lambda i: (i, 0)),
            _resident(w.shape, lambda i: (0, 0)),
        ],
        out_specs=pl.BlockSpec((TOKEN_TILE, d), lambda i: (i, 0)),
        out_shape=jax.ShapeDtypeStruct((t, d), _F32),
        compiler_params=_params("arbitrary"),
        name="out_proj",
    )(x, y, w)


def _attn_kernel(q_ref, k_ref, v_ref, zg_ref, tri_ref, y_ref, acc_ref):
    qi = pl.program_id(2)
    tq = q_ref.shape[0]
    q = q_ref[...]
    tri = tri_ref[...]
    q_pos = qi * tq + lax.broadcasted_iota(jnp.int32, (tq, 1), 0)

    def key_block(start, c, masked):
        start = pl.multiple_of(start, KEY_BLOCK)
        kb = k_ref[pl.ds(start, KEY_BLOCK), :]
        vb = v_ref[pl.ds(start, KEY_BLOCK), :]
        z = lax.dot_general(q, kb, (((1,), (1,)), ((), ())), preferred_element_type=_F32)
        lg = jnp.log(1.0 + jnp.exp(-jnp.abs(z)))
        sp = jnp.maximum(z, 0.0) + lg
        ls = jnp.minimum(z, 0.0) - lg
        if masked:
            valid = start + lax.broadcasted_iota(jnp.int32, (1, KEY_BLOCK), 1) < q_pos
            sp = jnp.where(valid, sp, 0.0)
        later = jnp.dot(sp.astype(_BF16), tri, preferred_element_type=_F32)
        p = jnp.exp(ls + (later + c))
        if masked:
            p = jnp.where(valid, p, 0.0)
        acc_ref[...] += jnp.dot(p.astype(_BF16), vb, preferred_element_type=_F32)
        return c - jnp.sum(sp, axis=1, keepdims=True)

    def key_chunk(base, c, masked):
        for kb in reversed(range(tq // KEY_BLOCK)):
            c = key_block(base + kb * KEY_BLOCK, c, masked)
        return c

    acc_ref[...] = jnp.zeros(acc_ref.shape, _F32)
    c = key_chunk(qi * tq, jnp.zeros((tq, 1), _F32), masked=True)
    lax.fori_loop(0, qi, lambda j, c: key_chunk((qi - 1 - j) * tq, c, masked=False), c)
    y_ref[...] = (acc_ref[...] * _silu(zg_ref[...])).astype(y_ref.dtype)


def _attention(q, k, v, zg):
    b, s, w = q.shape
    dh = w // N_HEADS
    assert s % QUERY_TILE == 0 and QUERY_TILE % KEY_BLOCK == 0
    row = lax.broadcasted_iota(jnp.int32, (KEY_BLOCK, KEY_BLOCK), 0)
    col = lax.broadcasted_iota(jnp.int32, (KEY_BLOCK, KEY_BLOCK), 1)
    tri = jnp.where(row > col, -1.0, 0.0).astype(_BF16)
    q_tile = pl.BlockSpec((None, QUERY_TILE, dh), lambda bi, hi, qi: (bi, qi, hi))
    kv_all = pl.BlockSpec((None, s, dh), lambda bi, hi, qi: (bi, 0, hi), pipeline_mode=pl.Buffered(1))
    return pl.pallas_call(
        _attn_kernel,
        grid=(b, N_HEADS, s // QUERY_TILE),
        in_specs=[
            q_tile,
            kv_all,
            kv_all,
            q_tile,
            _resident(tri.shape, lambda bi, hi, qi: (0, 0)),
        ],
        out_specs=q_tile,
        out_shape=jax.ShapeDtypeStruct((b, s, w), _BF16),
        scratch_shapes=[pltpu.VMEM((QUERY_TILE, dh), _F32)],
        compiler_params=_params("arbitrary", "arbitrary", "arbitrary"),
        name="stick_attention",
    )(q, k, v, zg, tri)


def _final_norm_kernel(x_ref, g_ref, o_ref):
    o_ref[...] = _rmsnorm(x_ref[...], g_ref[...])


def _final_norm(x, g):
    t, d = x.shape
    return pl.pallas_call(
        _final_norm_kernel,
        grid=(t // TOKEN_TILE,),
        in_specs=[pl.BlockSpec((TOKEN_TILE, d), lambda i: (i, 0)), _resident((1, d), lambda i: (0, 0))],
        out_specs=pl.BlockSpec((TOKEN_TILE, d), lambda i: (i, 0)),
        out_shape=jax.ShapeDtypeStruct((t, d), _F32),
        compiler_params=_params("arbitrary"),
        name="final_norm",
    )(x, g.reshape(1, d))


def kernel(x, a_norm, a_w_in, a_w_grp, a_scale, a_w_out, kv_norm, w_kv, b_norm, b_w_in, b_w_out, final_norm):
    b, s, d = x.shape
    t = b * s
    w_b = b_w_in.shape[-1] // 2
    inv_sqrt_dh = float((w_b // N_HEADS) ** -0.5)
    xt = x.reshape(t, d)

    for i in range(a_norm.shape[0]):
        u, z = _norm_proj(xt, a_norm[i], a_w_in[i].astype(_BF16), _F32, _F32)
        w_a = u.shape[-1]
        y = _pool_gate(u.reshape(b, s, w_a), z.reshape(b, s, w_a), a_w_grp[i].astype(_BF16), a_scale[i])
        xt = _out_proj(xt, y.reshape(t, w_a), a_w_out[i].astype(_BF16))

    k, v = _norm_proj(xt, kv_norm, w_kv.astype(_BF16), _BF16, _BF16)
    k = k.reshape(b, s, w_b)
    v = v.reshape(b, s, w_b)

    for j in range(b_norm.shape[0]):
        q, zg = _norm_proj(xt, b_norm[j], b_w_in[j].astype(_BF16), _BF16, _F32, scale1=inv_sqrt_dh)
        y = _attention(q.reshape(b, s, w_b), k, v, zg.reshape(b, s, w_b))
        xt = _out_proj(xt, y.reshape(t, w_b), b_w_out[j].astype(_BF16))

    return _final_norm(xt, final_norm).reshape(b, s, d)
```

```python
import functools

import jax
import jax.numpy as jnp
from jax import lax
from jax.experimental import pallas as pl
from jax.experimental.pallas import tpu as pltpu

N_HEADS = 8
POOL_WINDOWS = (2, 4, 8, 16)
EPS = 1e-6
LOG2_E = 1.4426950408889634
MASKED_LOG2 = -1e30
MAX_LOGIT_LOG2 = 126.0
_NT_DIMS = (((1,), (1,)), ((), ()))

V7X_SUBLANES = 8
V7X_MXU_DIM = 256
V7X_VMEM_BYTES = 64 * 1024 * 1024
VMEM_LIMIT_BYTES = V7X_VMEM_BYTES - 8 * 1024 * 1024

POOL_HALO = -(-(max(POOL_WINDOWS) - 1) // V7X_SUBLANES) * V7X_SUBLANES
KEY_BLOCK = V7X_MXU_DIM

TOKEN_TILE = 512
QUERY_TILE = 1024
KEY_CHUNK = QUERY_TILE // 2
PROJ_COL_CHUNK = 512

_F32 = jnp.float32
_BF16 = jnp.bfloat16


def _resident(block_shape, index_map):
    return pl.BlockSpec(block_shape, index_map, pipeline_mode=pl.Buffered(1))


def _params(*semantics):
    return pltpu.CompilerParams(dimension_semantics=semantics, vmem_limit_bytes=VMEM_LIMIT_BYTES)


def _rmsnorm(x, g):
    ms = jnp.mean(x * x, axis=-1, keepdims=True)
    return x * lax.rsqrt(ms + EPS) * g


def _silu(z):
    return z * jax.nn.sigmoid(z)


def _norm_proj_kernel(x_ref, g_ref, w_ref, o1_ref, o2_ref, h_ref, *, scale1):
    h_ref[...] = _rmsnorm(x_ref[...], g_ref[...]).astype(_BF16)
    n_half = o1_ref.shape[-1]
    for o_ref, col0, scale in ((o1_ref, 0, scale1), (o2_ref, n_half, 1.0)):
        for c in range(n_half // PROJ_COL_CHUNK):
            lo = c * PROJ_COL_CHUNK
            r = jnp.dot(h_ref[...], w_ref[:, col0 + lo:col0 + lo + PROJ_COL_CHUNK],
                        preferred_element_type=_F32)
            if scale != 1.0:
                r = r * scale
            o_ref[:, lo:lo + PROJ_COL_CHUNK] = r.astype(o_ref.dtype)


def _norm_proj(x, g, w, dtype1, dtype2, scale1=1.0):
    t, d = x.shape
    n_half = w.shape[1] // 2
    assert t % TOKEN_TILE == 0 and n_half % PROJ_COL_CHUNK == 0
    return pl.pallas_call(
        functools.partial(_norm_proj_kernel, scale1=scale1),
        grid=(t // TOKEN_TILE,),
        in_specs=[
            pl.BlockSpec((TOKEN_TILE, d), lambda i: (i, 0)),
            _resident((1, d), lambda i: (0, 0)),
            _resident(w.shape, lambda i: (0, 0)),
        ],
        out_specs=[
            pl.BlockSpec((TOKEN_TILE, n_half), lambda i: (i, 0)),
            pl.BlockSpec((TOKEN_TILE, n_half), lambda i: (i, 0)),
        ],
        out_shape=[jax.ShapeDtypeStruct((t, n_half), dtype1), jax.ShapeDtypeStruct((t, n_half), dtype2)],
        scratch_shapes=[pltpu.VMEM((TOKEN_TILE, d), _BF16)],
        compiler_params=_params("arbitrary"),
        name="norm_proj",
    )(x, g.reshape(1, d), w)


def _pool_gate_kernel(u_ref, z_ref, wg_ref, sc_ref, y_ref, ubuf):
    s = pl.program_id(1)
    tt = u_ref.shape[0]
    g = wg_ref.shape[-1]

    @pl.when(s == 0)
    def _():
        ubuf[0:POOL_HALO, :] = jnp.zeros((POOL_HALO, ubuf.shape[1]), _F32)

    @pl.when(s > 0)
    def _():
        ubuf[0:POOL_HALO, :] = ubuf[tt:tt + POOL_HALO, :]

    ubuf[POOL_HALO:POOL_HALO + tt, :] = u_ref[...]
    pos = s * tt + lax.broadcasted_iota(jnp.int32, (tt, 1), 0)
    for gi, w in enumerate(POOL_WINDOWS):
        cols = slice(gi * g, (gi + 1) * g)
        cur = ubuf[POOL_HALO:POOL_HALO + tt, cols]
        acc = cur
        for k in range(1, w):
            acc = acc + ubuf[POOL_HALO - k:POOL_HALO - k + tt, cols]
        cnt = jnp.minimum(pos + 1, w).astype(_F32)
        pm = acc / cnt - cur
        yg = jnp.dot(pm.astype(_BF16), wg_ref[gi], preferred_element_type=_F32)
        y_ref[:, cols] = (yg * sc_ref[:, cols] * _silu(z_ref[:, cols])).astype(y_ref.dtype)


def _pool_gate(u, z, w_grp, scale):
    b, s, w = u.shape
    assert s % TOKEN_TILE == 0 and TOKEN_TILE >= POOL_HALO
    tile = pl.BlockSpec((None, TOKEN_TILE, w), lambda bi, si: (bi, si, 0))
    return pl.pallas_call(
        _pool_gate_kernel,
        grid=(b, s // TOKEN_TILE),
        in_specs=[
            tile,
            tile,
            _resident(w_grp.shape, lambda bi, si: (0, 0, 0)),
            _resident((1, w), lambda bi, si: (0, 0)),
        ],
        out_specs=tile,
        out_shape=jax.ShapeDtypeStruct((b, s, w), _BF16),
        scratch_shapes=[pltpu.VMEM((POOL_HALO + TOKEN_TILE, w), _F32)],
        compiler_params=_params("arbitrary", "arbitrary"),
        name="pool_gate",
    )(u, z, w_grp, scale.reshape(1, w))


def _out_proj_kernel(x_ref, y_ref, w_ref, o_ref):
    o_ref[...] = x_ref[...] + jnp.dot(y_ref[...], w_ref[...], preferred_element_type=_F32)


def _out_proj(x, y, w):
    t, d = x.shape
    return pl.pallas_call(
        _out_proj_kernel,
        grid=(t // TOKEN_TILE,),
        in_specs=[
            pl.BlockSpec((TOKEN_TILE, d), lambda i: (i, 0)),
            pl.BlockSpec((TOKEN_TILE, w.shape[0]), lambda i: (i, 0)),
            _resident(w.shape, lambda i: (0, 0)),
        ],
        out_specs=pl.BlockSpec((TOKEN_TILE, d), lambda i: (i, 0)),
        out_shape=jax.ShapeDtypeStruct((t, d), _F32),
        compiler_params=_params("arbitrary"),
        name="out_proj",
    )(x, y, w)


def _attn_kernel(q_ref, k_ref, v_ref, zg_ref, tri_ref, y_ref, acc_ref, sp0, ls0, sp1, ls1):
    qi = pl.program_id(2)
    tq = q_ref.shape[0]
    ck = sp0.shape[1]
    assert tq == 2 * ck
    q_pos = qi * tq + lax.broadcasted_iota(jnp.int32, (tq, 1), 0)

    def chunk_start(n):
        return pl.multiple_of((qi + 1) * tq - (n + 1) * ck, ck)

    def logits_stage(n, sp_ref, ls_ref, c, masked):
        start = chunk_start(n)
        z = lax.dot_general(q_ref[...], k_ref[pl.ds(start, ck), :], _NT_DIMS, preferred_element_type=_F32)
        z = jnp.minimum(z, MAX_LOGIT_LOG2)
        sp = jnp.log(1.0 + jnp.exp2(z)) * LOG2_E
        ls = z - sp
        if masked:
            valid = start + lax.broadcasted_iota(jnp.int32, (1, ck), 1) < q_pos
            sp = jnp.where(valid, sp, 0.0)
            ls = jnp.where(valid, ls, MASKED_LOG2)
        sp_ref[...] = sp.astype(sp_ref.dtype)
        for kb in reversed(range(ck // KEY_BLOCK)):
            cols = slice(kb * KEY_BLOCK, (kb + 1) * KEY_BLOCK)
            ls_ref[:, cols] = ls[:, cols] + c
            c = c - jnp.sum(sp[:, cols], axis=1, keepdims=True)
        return c

    def values_stage(n, sp_ref, ls_ref):
        p = []
        for kb in range(ck // KEY_BLOCK):
            cols = slice(kb * KEY_BLOCK, (kb + 1) * KEY_BLOCK)
            later = jnp.dot(sp_ref[:, cols], tri_ref[...], preferred_element_type=_F32)
            p.append(jnp.exp2(ls_ref[:, cols] + later).astype(_BF16))
        acc_ref[...] += jnp.dot(jnp.concatenate(p, axis=1), v_ref[pl.ds(chunk_start(n), ck), :],
                                preferred_element_type=_F32)

    acc_ref[...] = jnp.zeros(acc_ref.shape, _F32)
    c = logits_stage(0, sp0, ls0, jnp.zeros((tq, 1), _F32), masked=True)
    c = logits_stage(1, sp1, ls1, c, masked=True)
    values_stage(0, sp0, ls0)

    def chunk_pair(n, c):
        c = logits_stage(n + 1, sp0, ls0, c, masked=False)
        values_stage(n, sp1, ls1)
        c = logits_stage(n + 2, sp1, ls1, c, masked=False)
        values_stage(n + 1, sp0, ls0)
        return c

    c = lax.fori_loop(0, qi // 2, lambda m, c: chunk_pair(4 * m + 3, chunk_pair(4 * m + 1, c)), c)

    @pl.when(qi % 2 == 1)
    def _():
        chunk_pair(2 * qi - 1, c)

    values_stage(2 * qi + 1, sp1, ls1)
    y_ref[...] = (acc_ref[...] * _silu(zg_ref[...])).astype(y_ref.dtype)


def _attention(q, k, v, zg):
    b, s, w = q.shape
    dh = w // N_HEADS
    assert s % QUERY_TILE == 0 and KEY_CHUNK % KEY_BLOCK == 0
    row = lax.broadcasted_iota(jnp.int32, (KEY_BLOCK, KEY_BLOCK), 0)
    col = lax.broadcasted_iota(jnp.int32, (KEY_BLOCK, KEY_BLOCK), 1)
    tri = jnp.where(row > col, -1.0, 0.0).astype(_BF16)
    q_tile = pl.BlockSpec((None, QUERY_TILE, dh), lambda bi, hi, qi: (bi, qi, hi))
    kv_all = pl.BlockSpec((None, s, dh), lambda bi, hi, qi: (bi, 0, hi), pipeline_mode=pl.Buffered(1))
    chunk = (QUERY_TILE, KEY_CHUNK)
    return pl.pallas_call(
        _attn_kernel,
        grid=(b, N_HEADS, s // QUERY_TILE),
        in_specs=[
            q_tile,
            kv_all,
            kv_all,
            q_tile,
            _resident(tri.shape, lambda bi, hi, qi: (0, 0)),
        ],
        out_specs=q_tile,
        out_shape=jax.ShapeDtypeStruct((b, s, w), _BF16),
        scratch_shapes=[
            pltpu.VMEM((QUERY_TILE, dh), _F32),
            pltpu.VMEM(chunk, _BF16), pltpu.VMEM(chunk, _F32),
            pltpu.VMEM(chunk, _BF16), pltpu.VMEM(chunk, _F32),
        ],
        compiler_params=_params("arbitrary", "arbitrary", "arbitrary"),
        name="stick_attention",
    )(q, k, v, zg, tri)


def _final_norm_kernel(x_ref, g_ref, o_ref):
    o_ref[...] = _rmsnorm(x_ref[...], g_ref[...])


def _final_norm(x, g):
    t, d = x.shape
    return pl.pallas_call(
        _final_norm_kernel,
        grid=(t // TOKEN_TILE,),
        in_specs=[pl.BlockSpec((TOKEN_TILE, d), lambda i: (i, 0)), _resident((1, d), lambda i: (0, 0))],
        out_specs=pl.BlockSpec((TOKEN_TILE, d), lambda i: (i, 0)),
        out_shape=jax.ShapeDtypeStruct((t, d), _F32),
        compiler_params=_params("arbitrary"),
        name="final_norm",
    )(x, g.reshape(1, d))


def kernel(x, a_norm, a_w_in, a_w_grp, a_scale, a_w_out, kv_norm, w_kv, b_norm, b_w_in, b_w_out, final_norm):
    b, s, d = x.shape
    t = b * s
    w_b = b_w_in.shape[-1] // 2
    inv_sqrt_dh = float((w_b // N_HEADS) ** -0.5)
    xt = x.reshape(t, d)

    for i in range(a_norm.shape[0]):
        u, z = _norm_proj(xt, a_norm[i], a_w_in[i].astype(_BF16), _F32, _F32)
        w_a = u.shape[-1]
        y = _pool_gate(u.reshape(b, s, w_a), z.reshape(b, s, w_a), a_w_grp[i].astype(_BF16), a_scale[i])
        xt = _out_proj(xt, y.reshape(t, w_a), a_w_out[i].astype(_BF16))

    k, v = _norm_proj(xt, kv_norm, w_kv.astype(_BF16), _BF16, _BF16)
    k = k.reshape(b, s, w_b)
    v = v.reshape(b, s, w_b)

    for j in range(b_norm.shape[0]):
        q, zg = _norm_proj(xt, b_norm[j], b_w_in[j].astype(_BF16), _BF16, _F32, scale1=inv_sqrt_dh * LOG2_E)
        y = _attention(q.reshape(b, s, w_b), k, v, zg.reshape(b, s, w_b))
        xt = _out_proj(xt, y.reshape(t, w_b), b_w_out[j].astype(_BF16))

    return _final_norm(xt, final_norm).reshape(b, s, d)
```

```python
import functools

import jax
import jax.numpy as jnp
from jax import lax
from jax.experimental import pallas as pl
from jax.experimental.pallas import tpu as pltpu

N_HEADS = 8
POOL_WINDOWS = (2, 4, 8, 16)
EPS = 1e-6
LOG2_E = 1.4426950408889634
MASKED_LOG2 = -1e30
MAX_LOGIT_LOG2 = 126.0
_NT_DIMS = (((1,), (1,)), ((), ()))

V7X_SUBLANES = 8
V7X_MXU_DIM = 256
V7X_VMEM_BYTES = 64 * 1024 * 1024
VMEM_LIMIT_BYTES = V7X_VMEM_BYTES - 8 * 1024 * 1024

POOL_HALO = -(-(max(POOL_WINDOWS) - 1) // V7X_SUBLANES) * V7X_SUBLANES
KEY_BLOCK = V7X_MXU_DIM

TOKEN_TILE = 512
QUERY_TILE = 1024
KEY_CHUNK = QUERY_TILE // 2
PROJ_COL_CHUNK = 512

_F32 = jnp.float32
_BF16 = jnp.bfloat16


def _resident(block_shape, index_map):
    return pl.BlockSpec(block_shape, index_map, pipeline_mode=pl.Buffered(1))


def _params(*semantics):
    return pltpu.CompilerParams(dimension_semantics=semantics, vmem_limit_bytes=VMEM_LIMIT_BYTES)


def _rmsnorm(x, g):
    ms = jnp.mean(x * x, axis=-1, keepdims=True)
    return x * lax.rsqrt(ms + EPS) * g


def _silu(z):
    return z * jax.nn.sigmoid(z)


def _norm_proj_kernel(x_ref, g_ref, w_ref, o1_ref, o2_ref, h_ref, *, scale1):
    h_ref[...] = _rmsnorm(x_ref[...], g_ref[...]).astype(_BF16)
    n_half = o1_ref.shape[-1]
    for o_ref, col0, scale in ((o1_ref, 0, scale1), (o2_ref, n_half, 1.0)):
        for c in range(n_half // PROJ_COL_CHUNK):
            lo = c * PROJ_COL_CHUNK
            r = jnp.dot(h_ref[...], w_ref[:, col0 + lo:col0 + lo + PROJ_COL_CHUNK],
                        preferred_element_type=_F32)
            if scale != 1.0:
                r = r * scale
            o_ref[:, lo:lo + PROJ_COL_CHUNK] = r.astype(o_ref.dtype)


def _norm_proj(x, g, w, dtype1, dtype2, scale1=1.0):
    t, d = x.shape
    n_half = w.shape[1] // 2
    assert t % TOKEN_TILE == 0 and n_half % PROJ_COL_CHUNK == 0
    return pl.pallas_call(
        functools.partial(_norm_proj_kernel, scale1=scale1),
        grid=(t // TOKEN_TILE,),
        in_specs=[
            pl.BlockSpec((TOKEN_TILE, d), lambda i: (i, 0)),
            _resident((1, d), lambda i: (0, 0)),
            _resident(w.shape, lambda i: (0, 0)),
        ],
        out_specs=[
            pl.BlockSpec((TOKEN_TILE, n_half), lambda i: (i, 0)),
            pl.BlockSpec((TOKEN_TILE, n_half), lambda i: (i, 0)),
        ],
        out_shape=[jax.ShapeDtypeStruct((t, n_half), dtype1), jax.ShapeDtypeStruct((t, n_half), dtype2)],
        scratch_shapes=[pltpu.VMEM((TOKEN_TILE, d), _BF16)],
        compiler_params=_params("arbitrary"),
        name="norm_proj",
    )(x, g.reshape(1, d), w)


def _pool_front_kernel(x_ref, g_ref, w_ref, wg_ref, sc_ref, y_ref, h_ref, ubuf, lvl_a, lvl_b):
    s = pl.program_id(1)
    tt = x_ref.shape[0]
    n_grp, grp = wg_ref.shape[0], wg_ref.shape[-1]
    w_a = n_grp * grp
    pad = V7X_SUBLANES
    tile0 = pad + POOL_HALO
    end = tile0 + tt

    @pl.when(s == 0)
    def _():
        ubuf[0:tile0, :] = jnp.zeros((tile0, ubuf.shape[1]), _F32)
        lvl_a[0:pad, :] = jnp.zeros((pad, grp), _F32)
        lvl_b[0:pad, :] = jnp.zeros((pad, grp), _F32)

    @pl.when(s > 0)
    def _():
        ubuf[pad:tile0, :] = ubuf[end - POOL_HALO:end, :]

    h_ref[...] = _rmsnorm(x_ref[...], g_ref[...]).astype(_BF16)
    pos = s * tt + lax.broadcasted_iota(jnp.int32, (tt, 1), 0)
    for gi, w in enumerate(POOL_WINDOWS):
        cols = slice(gi * grp, (gi + 1) * grp)
        ubuf[tile0:end, cols] = jnp.dot(h_ref[...], w_ref[:, cols], preferred_element_type=_F32)
        level = ubuf[pad:end, cols] + ubuf[pad - 1:end - 1, cols]
        shift, bufs = 2, [lvl_a, lvl_b]
        while shift < w:
            buf = bufs[0]
            bufs.reverse()
            buf[pad:end, :] = level
            level = level + buf[pad - shift:end - shift, :]
            shift *= 2
        cur = ubuf[tile0:end, cols]
        cnt = jnp.minimum(pos + 1, w).astype(_F32)
        pm = level[POOL_HALO:, :] / cnt - cur
        yg = jnp.dot(pm.astype(_BF16), wg_ref[gi], preferred_element_type=_F32)
        z = jnp.dot(h_ref[...], w_ref[:, w_a + gi * grp:w_a + (gi + 1) * grp], preferred_element_type=_F32)
        y_ref[:, cols] = (yg * sc_ref[:, cols] * _silu(z)).astype(y_ref.dtype)


def _pool_front(x, g, w_in, w_grp, scale):
    b, s, d = x.shape
    w_a = w_in.shape[1] // 2
    grp = w_grp.shape[-1]
    assert len(POOL_WINDOWS) == w_grp.shape[0] and w_a == w_grp.shape[0] * grp
    assert all(w >= 2 and w & (w - 1) == 0 for w in POOL_WINDOWS)
    assert s % TOKEN_TILE == 0 and TOKEN_TILE >= POOL_HALO
    rows = V7X_SUBLANES + POOL_HALO + TOKEN_TILE
    return pl.pallas_call(
        _pool_front_kernel,
        grid=(b, s // TOKEN_TILE),
        in_specs=[
            pl.BlockSpec((None, TOKEN_TILE, d), lambda bi, si: (bi, si, 0)),
            _resident((1, d), lambda bi, si: (0, 0)),
            _resident(w_in.shape, lambda bi, si: (0, 0)),
            _resident(w_grp.shape, lambda bi, si: (0, 0, 0)),
            _resident((1, w_a), lambda bi, si: (0, 0)),
        ],
        out_specs=pl.BlockSpec((None, TOKEN_TILE, w_a), lambda bi, si: (bi, si, 0)),
        out_shape=jax.ShapeDtypeStruct((b, s, w_a), _BF16),
        scratch_shapes=[pltpu.VMEM((TOKEN_TILE, d), _BF16), pltpu.VMEM((rows, w_a), _F32),
                        pltpu.VMEM((rows, grp), _F32), pltpu.VMEM((rows, grp), _F32)],
        compiler_params=_params("arbitrary", "arbitrary"),
        name="pool_front",
    )(x, g.reshape(1, d), w_in, w_grp, scale.reshape(1, w_a))


def _out_proj_kernel(x_ref, y_ref, w_ref, o_ref):
    o_ref[...] = x_ref[...] + jnp.dot(y_ref[...], w_ref[...], preferred_element_type=_F32)


def _out_proj_norm_kernel(x_ref, y_ref, w_ref, g_ref, o_ref):
    o_ref[...] = _rmsnorm(x_ref[...] + jnp.dot(y_ref[...], w_ref[...], preferred_element_type=_F32), g_ref[...])


def _out_proj(x, y, w, final_norm=None):
    t, d = x.shape
    in_specs = [
        pl.BlockSpec((TOKEN_TILE, d), lambda i: (i, 0)),
        pl.BlockSpec((TOKEN_TILE, w.shape[0]), lambda i: (i, 0)),
        _resident(w.shape, lambda i: (0, 0)),
    ]
    args = (x, y, w)
    if final_norm is not None:
        in_specs.append(_resident((1, d), lambda i: (0, 0)))
        args += (final_norm.reshape(1, d),)
    return pl.pallas_call(
        _out_proj_kernel if final_norm is None else _out_proj_norm_kernel,
        grid=(t // TOKEN_TILE,),
        in_specs=in_specs,
        out_specs=pl.BlockSpec((TOKEN_TILE, d), lambda i: (i, 0)),
        out_shape=jax.ShapeDtypeStruct((t, d), _F32),
        compiler_params=_params("arbitrary"),
        name="out_proj",
    )(*args)


def _attn_kernel(q_ref, k_ref, v_ref, zg_ref, tri_ref, y_ref, acc_ref, sp0, ls0, sp1, ls1):
    qi = pl.program_id(2)
    tq = q_ref.shape[0]
    ck = sp0.shape[1]
    assert tq == 2 * ck
    q_pos = qi * tq + lax.broadcasted_iota(jnp.int32, (tq, 1), 0)

    def chunk_start(n):
        return pl.multiple_of((qi + 1) * tq - (n + 1) * ck, ck)

    def logits_stage(n, sp_ref, ls_ref, cs, masked, row0=0):
        rows = slice(row0, tq)
        start = chunk_start(n)
        z = lax.dot_general(q_ref[rows, :], k_ref[pl.ds(start, ck), :], _NT_DIMS, preferred_element_type=_F32)
        z = jnp.minimum(z, MAX_LOGIT_LOG2)
        sp = jnp.log(1.0 + jnp.exp2(z)) * LOG2_E
        ls = z - sp
        if masked:
            valid = start + lax.broadcasted_iota(jnp.int32, (1, ck), 1) < q_pos[rows]
            sp = jnp.where(valid, sp, 0.0)
            ls = jnp.where(valid, ls, MASKED_LOG2)
        sp_ref[rows, :] = sp.astype(sp_ref.dtype)
        cs = list(cs)
        for band in range(len(cs)):
            local = slice(band * ck, (band + 1) * ck)
            stored = slice(row0 + band * ck, row0 + (band + 1) * ck)
            for kb in reversed(range(ck // KEY_BLOCK)):
                cols = slice(kb * KEY_BLOCK, (kb + 1) * KEY_BLOCK)
                ls_ref[stored, cols] = ls[local, cols] + cs[band]
                cs[band] = cs[band] - jnp.sum(sp[local, cols], axis=1, keepdims=True)
        return tuple(cs)

    def values_stage(n, sp_ref, ls_ref, row0=0):
        rows = slice(row0, tq)
        p = []
        for kb in range(ck // KEY_BLOCK):
            cols = slice(kb * KEY_BLOCK, (kb + 1) * KEY_BLOCK)
            later = jnp.dot(sp_ref[rows, cols], tri_ref[...], preferred_element_type=_F32)
            p.append(jnp.exp2(ls_ref[rows, cols] + later).astype(_BF16))
        acc_ref[rows, :] += jnp.dot(jnp.concatenate(p, axis=1), v_ref[pl.ds(chunk_start(n), ck), :],
                                    preferred_element_type=_F32)

    acc_ref[...] = jnp.zeros(acc_ref.shape, _F32)
    zero = jnp.zeros((ck, 1), _F32)
    c = (zero,) + logits_stage(0, sp0, ls0, (zero,), masked=True, row0=ck)
    c = logits_stage(1, sp1, ls1, c, masked=True)
    values_stage(0, sp0, ls0, row0=ck)

    def chunk_pair(n, c):
        c = logits_stage(n + 1, sp0, ls0, c, masked=False)
        values_stage(n, sp1, ls1)
        c = logits_stage(n + 2, sp1, ls1, c, masked=False)
        values_stage(n + 1, sp0, ls0)
        return c

    c = lax.fori_loop(0, qi // 2, lambda m, c: chunk_pair(4 * m + 3, chunk_pair(4 * m + 1, c)), c)

    @pl.when(qi % 2 == 1)
    def _():
        chunk_pair(2 * qi - 1, c)

    values_stage(2 * qi + 1, sp1, ls1)
    y_ref[...] = (acc_ref[...] * _silu(zg_ref[...])).astype(y_ref.dtype)


def _attention(q, k, v, zg):
    b, s, w = q.shape
    dh = w // N_HEADS
    assert s % QUERY_TILE == 0 and KEY_CHUNK % KEY_BLOCK == 0
    row = lax.broadcasted_iota(jnp.int32, (KEY_BLOCK, KEY_BLOCK), 0)
    col = lax.broadcasted_iota(jnp.int32, (KEY_BLOCK, KEY_BLOCK), 1)
    tri = jnp.where(row > col, -1.0, 0.0).astype(_BF16)
    q_tile = pl.BlockSpec((None, QUERY_TILE, dh), lambda bi, hi, qi: (bi, qi, hi))
    k_all = pl.BlockSpec((None, s, dh), lambda bi, hi, qi: (bi, 0, hi), pipeline_mode=pl.Buffered(1))
    v_all = pl.BlockSpec((None, s, dh), lambda bi, hi, qi: (bi, 0, hi), pipeline_mode=pl.Buffered(1))
    chunk = (QUERY_TILE, KEY_CHUNK)
    return pl.pallas_call(
        _attn_kernel,
        grid=(b, N_HEADS, s // QUERY_TILE),
        in_specs=[
            q_tile,
            k_all,
            v_all,
            q_tile,
            _resident(tri.shape, lambda bi, hi, qi: (0, 0)),
        ],
        out_specs=q_tile,
        out_shape=jax.ShapeDtypeStruct((b, s, w), _BF16),
        scratch_shapes=[
            pltpu.VMEM((QUERY_TILE, dh), _F32),
            pltpu.VMEM(chunk, _BF16), pltpu.VMEM(chunk, _F32),
            pltpu.VMEM(chunk, _BF16), pltpu.VMEM(chunk, _F32),
        ],
        compiler_params=_params("arbitrary", "arbitrary", "arbitrary"),
        name="stick_attention",
    )(q, k, v, zg, tri)


def kernel(x, a_norm, a_w_in, a_w_grp, a_scale, a_w_out, kv_norm, w_kv, b_norm, b_w_in, b_w_out, final_norm):
    b, s, d = x.shape
    t = b * s
    w_b = b_w_in.shape[-1] // 2
    inv_sqrt_dh = float((w_b // N_HEADS) ** -0.5)
    n_b = b_norm.shape[0]
    assert n_b >= 1
    xt = x.reshape(t, d)

    for i in range(a_norm.shape[0]):
        y = _pool_front(xt.reshape(b, s, d), a_norm[i], a_w_in[i].astype(_BF16), a_w_grp[i].astype(_BF16),
                        a_scale[i])
        xt = _out_proj(xt, y.reshape(t, -1), a_w_out[i].astype(_BF16))

    k, v = _norm_proj(xt, kv_norm, w_kv.astype(_BF16), _BF16, _BF16)
    k = k.reshape(b, s, w_b)
    v = v.reshape(b, s, w_b)

    for j in range(n_b):
        q, zg = _norm_proj(xt, b_norm[j], b_w_in[j].astype(_BF16), _BF16, _F32, scale1=inv_sqrt_dh * LOG2_E)
        y = _attention(q.reshape(b, s, w_b), k, v, zg.reshape(b, s, w_b))
        xt = _out_proj(xt, y.reshape(t, w_b), b_w_out[j].astype(_BF16),
                       final_norm=final_norm if j == n_b - 1 else None)

    return xt.reshape(b, s, d)
```
